```python
import jax, jax.numpy as jnp
from jax import lax
import numpy as np

D_MODEL = 2048
BATCH = 16
SEQ = 2048
DEPTH = 4

N_MIXERS = 2
N_A = (DEPTH + N_MIXERS - 1) // N_MIXERS
N_B = DEPTH // N_MIXERS
EPS = 1e-6
ROPE_THETA = 10000.0

A_HEADS = 16
A_HEAD_DIM = D_MODEL // A_HEADS
A_Q_WIDTH = A_HEADS * A_HEAD_DIM
IDX_HEADS = 16
IDX_DIM = 64
TOPK_MAX = 256
Q_BLOCK = 128
A_OFF_K = A_Q_WIDTH
A_OFF_V = A_OFF_K + A_HEAD_DIM
A_OFF_QI = A_OFF_V + A_HEAD_DIM
A_OFF_KI = A_OFF_QI + IDX_HEADS * IDX_DIM
A_OFF_WI = A_OFF_KI + IDX_DIM
A_IN = A_OFF_WI + IDX_HEADS

B_HEADS = 8
B_V_DIM = D_MODEL // B_HEADS
B_QK_DIM = B_V_DIM // 2
CHUNK = 64
B_OFF_K = B_HEADS * B_QK_DIM
B_OFF_V = 2 * B_HEADS * B_QK_DIM
B_OFF_O = B_OFF_V + B_HEADS * B_V_DIM
B_OFF_G = B_OFF_O + B_HEADS * B_V_DIM
B_IN = B_OFF_G + 2 * B_HEADS
M_INIT = -1e30

D_FF = 5632
CONV_WIDTH = 3

kernel_name = "hybrid_dsa_mlstm_convffn"


def rmsnorm(x, gain):
    xf = x.astype(jnp.float32)
    y = xf * lax.rsqrt(jnp.mean(xf * xf, axis=-1, keepdims=True) + EPS)
    return (y * gain.astype(jnp.float32)).astype(x.dtype)


def rope_tables(seq, dim):
    inv = 1.0 / (ROPE_THETA ** (jnp.arange(0, dim, 2, dtype=jnp.float32) / dim))
    ang = jnp.arange(seq, dtype=jnp.float32)[:, None] * inv[None, :]
    return jnp.cos(ang), jnp.sin(ang)


def apply_rope(x, cos, sin):
    xf = x.astype(jnp.float32)
    half = xf.shape[-1] // 2
    x1, x2 = xf[..., :half], xf[..., half:]
    c = cos[None, :, None, :]
    s = sin[None, :, None, :]
    return jnp.concatenate([x1 * c - x2 * s, x1 * s + x2 * c], axis=-1).astype(x.dtype)


def dsa_mixer(h, w_in, q_gain, k_gain, idx_k_gain, w_out, cos_a, sin_a, cos_i, sin_i):
    bsz, seq, _ = h.shape
    proj = h @ w_in
    q = proj[..., :A_OFF_K].reshape(bsz, seq, A_HEADS, A_HEAD_DIM)
    k = proj[..., A_OFF_K:A_OFF_V].reshape(bsz, seq, 1, A_HEAD_DIM)
    v = proj[..., A_OFF_V:A_OFF_QI]
    qi = proj[..., A_OFF_QI:A_OFF_KI].reshape(bsz, seq, IDX_HEADS, IDX_DIM)
    ki = proj[..., A_OFF_KI:A_OFF_WI].reshape(bsz, seq, 1, IDX_DIM)
    wi = proj[..., A_OFF_WI:].astype(jnp.float32) * (IDX_HEADS ** -0.5 * IDX_DIM ** -0.5)
    q = apply_rope(rmsnorm(q, q_gain), cos_a, sin_a)
    k = apply_rope(rmsnorm(k, k_gain), cos_a, sin_a)[:, :, 0]
    qi = apply_rope(qi, cos_i, sin_i)
    ki = apply_rope(rmsnorm(ki, idx_k_gain), cos_i, sin_i)[:, :, 0]

    n_keep = min(TOPK_MAX, seq // 4)
    qb = min(Q_BLOCK, seq)
    nblk = seq // qb
    key_pos = jnp.arange(seq, dtype=jnp.int32)

    def blocks(t):
        return jnp.swapaxes(t.reshape(bsz, nblk, qb, *t.shape[2:]), 0, 1)

    def attend_block(args):
        q_blk, qi_blk, wi_blk, q_pos = args
        causal = key_pos[None, None, :] <= q_pos[None, :, None]
        rel = jax.nn.relu(jnp.einsum('bqhd,bsd->bqhs', qi_blk, ki))
        score = jnp.einsum('bqhs,bqh->bqs', rel, wi_blk).astype(jnp.float32)
        score = jnp.where(causal, score, -jnp.inf)
        _, sel = lax.top_k(score, n_keep)
        valid = sel <= q_pos[None, :, None]
        k_sel = jax.vmap(lambda kb, ib: kb[ib])(k, sel)
        v_sel = jax.vmap(lambda vb, ib: vb[ib])(v, sel)
        logits = jnp.einsum('bqhd,bqkd->bqhk', q_blk, k_sel).astype(jnp.float32) * (A_HEAD_DIM ** -0.5)
        logits = jnp.where(valid[:, :, None, :], logits, -jnp.inf)
        p = jax.nn.softmax(logits, axis=-1).astype(v.dtype)
        return jnp.einsum('bqhk,bqkd->bqhd', p, v_sel)

    out = lax.map(attend_block, (blocks(q), blocks(qi), blocks(wi), key_pos.reshape(nblk, qb)))
    out = jnp.swapaxes(out, 0, 1).reshape(bsz, seq, A_Q_WIDTH)
    return out @ w_out


def to_chunks(t, nc, cl):
    t = t.reshape(t.shape[0], nc, cl, *t.shape[2:])
    return jnp.swapaxes(jnp.swapaxes(t, 0, 1), 2, 3)


def mlstm_chunkwise(q, k, v, log_i, log_f):
    bsz, seq, nh, dk = q.shape
    dv = v.shape[-1]
    cl = min(CHUNK, seq)
    nc = seq // cl
    tril = jnp.tril(jnp.ones((cl, cl), dtype=bool))

    def step(carry, inp):
        c_mat, n_vec, m = carry
        qj, kj, vj, li, lf = inp
        b = jnp.cumsum(lf, axis=-1)
        a = b + m[..., None]
        dmat = b[..., :, None] - b[..., None, :] + li[..., None, :]
        dmat = jnp.where(tril, dmat, -jnp.inf)
        m_t = jnp.maximum(a, jnp.max(dmat, axis=-1))
        inter = jnp.exp(a - m_t)
        s = jnp.einsum('bhtd,bhsd->bhts', qj, kj) * jnp.exp(dmat - m_t[..., None])
        num = inter[..., None] * jnp.einsum('bhtd,bhdv->bhtv', qj, c_mat) + jnp.einsum('bhts,bhsv->bhtv', s, vj)
        den = inter * jnp.einsum('bhtd,bhd->bht', qj, n_vec) + jnp.sum(s, axis=-1)
        hj = num / jnp.maximum(jnp.abs(den), jnp.exp(-m_t))[..., None]
        b_end = b[..., -1]
        g = b_end[..., None] - b + li
        m_new = jnp.maximum(b_end + m, jnp.max(g, axis=-1))
        decay = jnp.exp(b_end + m - m_new)
        wts = jnp.exp(g - m_new[..., None])
        c_new = decay[..., None, None] * c_mat + jnp.einsum('bhs,bhsd,bhsv->bhdv', wts, kj, vj)
        n_new = decay[..., None] * n_vec + jnp.einsum('bhs,bhsd->bhd', wts, kj)
        return (c_new, n_new, m_new), hj

    init = (jnp.zeros((bsz, nh, dk, dv), jnp.float32),
            jnp.zeros((bsz, nh, dk), jnp.float32),
            jnp.full((bsz, nh), M_INIT, jnp.float32))
    xs = (to_chunks(q, nc, cl), to_chunks(k, nc, cl), to_chunks(v, nc, cl),
          to_chunks(log_i, nc, cl), to_chunks(log_f, nc, cl))
    _, hs = lax.scan(step, init, xs)
    hs = jnp.swapaxes(jnp.swapaxes(hs, 2, 3), 0, 1)
    return hs.reshape(bsz, seq, nh, dv)


def mlstm_mixer(h, w_in, gate_bias, head_gain, w_out):
    bsz, seq, _ = h.shape
    proj = h @ w_in
    f32 = jnp.float32
    q = proj[..., :B_OFF_K].reshape(bsz, seq, B_HEADS, B_QK_DIM).astype(f32)
    k = proj[..., B_OFF_K:B_OFF_V].reshape(bsz, seq, B_HEADS, B_QK_DIM).astype(f32) * (B_QK_DIM ** -0.5)
    v = proj[..., B_OFF_V:B_OFF_O].reshape(bsz, seq, B_HEADS, B_V_DIM).astype(f32)
    o = jax.nn.sigmoid(proj[..., B_OFF_O:B_OFF_G].astype(f32))
    gates = proj[..., B_OFF_G:].astype(f32) + gate_bias.astype(f32)
    log_i = gates[..., :B_HEADS]
    log_f = jax.nn.log_sigmoid(gates[..., B_HEADS:])
    h_tilde = mlstm_chunkwise(q, k, v, log_i, log_f)
    h_norm = rmsnorm(h_tilde, head_gain.reshape(B_HEADS, B_V_DIM))
    h_out = (h_norm.reshape(bsz, seq, B_HEADS * B_V_DIM) * o).astype(h.dtype)
    return h_out @ w_out


def conv_ffn(h, w_up, conv_w, conv_b, w_down):
    up = h @ w_up
    gate, val = up[..., :D_FF], up[..., D_FF:]
    gate = lax.conv_general_dilated(
        gate, conv_w[:, None, :].astype(gate.dtype), window_strides=(1,),
        padding=[(CONV_WIDTH - 1, 0)], dimension_numbers=('NWC', 'WIO', 'NWC'),
        feature_group_count=D_FF) + conv_b
    return (jax.nn.gelu(gate, approximate=False) * val) @ w_down


def setup_inputs(seed: int = 0) -> dict:
    key = jax.random.key(seed)
    ks = jax.random.split(key, 20)

    def nrm(k, shape, scale):
        return jax.random.normal(k, shape, jnp.float32) * scale

    x = nrm(ks[0], (BATCH, SEQ, D_MODEL), 1.0)
    a_norm = 1.0 + nrm(ks[1], (N_A, D_MODEL), 0.02)
    a_w_in = nrm(ks[2], (N_A, D_MODEL, A_IN), D_MODEL ** -0.5)
    a_q_gain = 1.0 + nrm(ks[3], (N_A, A_HEAD_DIM), 0.02)
    a_k_gain = 1.0 + nrm(ks[4], (N_A, A_HEAD_DIM), 0.02)
    a_idx_k_gain = 1.0 + nrm(ks[5], (N_A, IDX_DIM), 0.02)
    a_w_out = nrm(ks[6], (N_A, A_Q_WIDTH, D_MODEL), A_Q_WIDTH ** -0.5)
    b_norm = 1.0 + nrm(ks[7], (N_B, D_MODEL), 0.02)
    b_w_in = nrm(ks[8], (N_B, D_MODEL, B_IN), D_MODEL ** -0.5)
    b_gate_bias = jnp.concatenate([nrm(ks[9], (N_B, B_HEADS), 0.1),
                                   3.0 + nrm(ks[10], (N_B, B_HEADS), 0.1)], axis=-1)
    b_head_norm = 1.0 + nrm(ks[11], (N_B, B_HEADS * B_V_DIM), 0.02)
    b_w_out = nrm(ks[12], (N_B, B_HEADS * B_V_DIM, D_MODEL), (B_HEADS * B_V_DIM) ** -0.5)
    f_norm = 1.0 + nrm(ks[13], (DEPTH, D_MODEL), 0.02)
    f_w_up = nrm(ks[14], (DEPTH, D_MODEL, 2 * D_FF), D_MODEL ** -0.5)
    f_conv_w = nrm(ks[15], (DEPTH, CONV_WIDTH, D_FF), CONV_WIDTH ** -0.5)
    f_conv_b = nrm(ks[16], (DEPTH, D_FF), 0.02)
    f_w_down = nrm(ks[17], (DEPTH, D_FF, D_MODEL), D_FF ** -0.5)
    return {"x": x, "a_norm": a_norm, "a_w_in": a_w_in, "a_q_gain": a_q_gain,
            "a_k_gain": a_k_gain, "a_idx_k_gain": a_idx_k_gain, "a_w_out": a_w_out,
            "b_norm": b_norm, "b_w_in": b_w_in, "b_gate_bias": b_gate_bias,
            "b_head_norm": b_head_norm, "b_w_out": b_w_out,
            "f_norm": f_norm, "f_w_up": f_w_up, "f_conv_w": f_conv_w,
            "f_conv_b": f_conv_b, "f_w_down": f_w_down}


def reference(x, a_norm, a_w_in, a_q_gain, a_k_gain, a_idx_k_gain, a_w_out,
              b_norm, b_w_in, b_gate_bias, b_head_norm, b_w_out,
              f_norm, f_w_up, f_conv_w, f_conv_b, f_w_down):
    seq = x.shape[1]
    cos_a, sin_a = rope_tables(seq, A_HEAD_DIM)
    cos_i, sin_i = rope_tables(seq, IDX_DIM)
    h = x
    for i in range(DEPTH):
        j = i // N_MIXERS
        if i % N_MIXERS == 0:
            h = h + dsa_mixer(rmsnorm(h, a_norm[j]), a_w_in[j], a_q_gain[j], a_k_gain[j],
                              a_idx_k_gain[j], a_w_out[j], cos_a, sin_a, cos_i, sin_i)
        else:
            h = h + mlstm_mixer(rmsnorm(h, b_norm[j]), b_w_in[j], b_gate_bias[j],
                                b_head_norm[j], b_w_out[j])
        h = h + conv_ffn(rmsnorm(h, f_norm[i]), f_w_up[i], f_conv_w[i], f_conv_b[i], f_w_down[i])
    return h
```

```python
import functools

import jax
import jax.numpy as jnp
import numpy as np
from jax import lax
from jax.experimental import pallas as pl
from jax.experimental.pallas import tpu as pltpu

EPS = 1e-6
ROPE_THETA = 10000.0
A_HEADS = 16
IDX_HEADS = 16
IDX_DIM = 64
TOPK_MAX = 256
B_HEADS = 8
CONV_WIDTH = 3
M_INIT = -1e30

LANES = 128
BF16_SUBLANES = 16
VMEM_LIMIT_BYTES = 56 * 1024 * 1024

MXU_DTYPE = jnp.bfloat16
ACT_DTYPE = jnp.bfloat16
NEG_BIG = -1e30

F32 = jnp.float32
I32 = jnp.int32
_NT = (((1,), (1,)), ((), ()))
_TN = (((0,), (0,)), ((), ()))


def _params(semantics):
    return pltpu.CompilerParams(dimension_semantics=semantics,
                                vmem_limit_bytes=VMEM_LIMIT_BYTES)


def _rms(x, gain):
    ms = jnp.mean(x * x, axis=-1, keepdims=True)
    return x * lax.rsqrt(ms + EPS) * gain


def _rmsnorm_kernel(x_ref, g_ref, o_ref):
    o_ref[...] = _rms(x_ref[...], g_ref[...]).astype(o_ref.dtype)


def rmsnorm(x, gain, *, tm=512):
    t, d = x.shape
    return pl.pallas_call(
        _rmsnorm_kernel,
        grid=(t // tm,),
        in_specs=[pl.BlockSpec((tm, d), lambda i: (i, 0)),
                  pl.BlockSpec((1, d), lambda i: (0, 0))],
        out_specs=pl.BlockSpec((tm, d), lambda i: (i, 0)),
        out_shape=jax.ShapeDtypeStruct((t, d), ACT_DTYPE),
        compiler_params=_params(("parallel",)),
        name="rmsnorm",
    )(x, gain.reshape(1, d))


def _proj_kernel(a_ref, w_ref, o_ref):
    o_ref[...] = jnp.dot(a_ref[...], w_ref[...],
                         preferred_element_type=F32).astype(o_ref.dtype)


def project(a, w, out_dtype, *, tm=1024, tn=512):
    t, k = a.shape
    n = w.shape[1]
    tm = min(tm, t)
    tn = min(tn, n)
    return pl.pallas_call(
        _proj_kernel,
        grid=(t // tm, n // tn),
        in_specs=[pl.BlockSpec((tm, k), lambda i, j: (i, 0)),
                  pl.BlockSpec((k, tn), lambda i, j: (0, j))],
        out_specs=pl.BlockSpec((tm, tn), lambda i, j: (i, j)),
        out_shape=jax.ShapeDtypeStruct((t, n), out_dtype),
        compiler_params=_params(("parallel", "arbitrary")),
        name="project",
    )(a, w)


def _out_proj_kernel(a_ref, w_ref, r_ref, g_ref, h_ref, hn_ref, acc_ref):
    kk = pl.program_id(1)

    @pl.when(kk == 0)
    def _():
        acc_ref[...] = r_ref[...]

    acc_ref[...] += jnp.dot(a_ref[...], w_ref[...], preferred_element_type=F32)

    @pl.when(kk == pl.num_programs(1) - 1)
    def _():
        h = acc_ref[...]
        h_ref[...] = h
        if hn_ref is not None:
            hn_ref[...] = _rms(h, g_ref[...]).astype(hn_ref.dtype)


def _out_proj_last_kernel(a_ref, w_ref, r_ref, h_ref, acc_ref):
    _out_proj_kernel(a_ref, w_ref, r_ref, None, h_ref, None, acc_ref)


def out_project(a, w, resid, next_gain, *, tm=512, tk=None):
    t, k = a.shape
    d = w.shape[1]
    if tk is None:
        tk = k if k <= 2048 else k // 4
    grid = (t // tm, k // tk)
    a_spec = pl.BlockSpec((tm, tk), lambda i, kk: (i, kk))
    w_spec = pl.BlockSpec((tk, d), lambda i, kk: (kk, 0))
    row_spec = pl.BlockSpec((tm, d), lambda i, kk: (i, 0))
    scratch = [pltpu.VMEM((tm, d), F32)]
    if next_gain is None:
        return pl.pallas_call(
            _out_proj_last_kernel, grid=grid,
            in_specs=[a_spec, w_spec, row_spec],
            out_specs=row_spec,
            out_shape=jax.ShapeDtypeStruct((t, d), F32),
            scratch_shapes=scratch,
            compiler_params=_params(("parallel", "arbitrary")),
            name="out_project_last",
        )(a, w, resid)
    return pl.pallas_call(
        _out_proj_kernel, grid=grid,
        in_specs=[a_spec, w_spec, row_spec,
                  pl.BlockSpec((1, d), lambda i, kk: (0, 0))],
        out_specs=[row_spec, row_spec],
        out_shape=[jax.ShapeDtypeStruct((t, d), F32),
                   jax.ShapeDtypeStruct((t, d), ACT_DTYPE)],
        scratch_shapes=scratch,
        compiler_params=_params(("parallel", "arbitrary")),
        name="out_project",
    )(a, w, resid, next_gain.reshape(1, d))


HALO = BF16_SUBLANES


def _ffn_up_kernel(a_ref, halo_ref, wg_ref, wv_ref, cw_ref, cb_ref, o_ref,
                   lhs_ref, *, tiles_per_seq):
    i = pl.program_id(0)
    tm = a_ref.shape[0]

    @pl.when(pl.program_id(1) == 0)
    def _():
        seq_start = (i % tiles_per_seq) == 0
        halo = halo_ref[...]
        lhs_ref[0:HALO, :] = jnp.where(seq_start, jnp.zeros_like(halo), halo)
        lhs_ref[HALO:, :] = a_ref[...]

    g = jnp.dot(lhs_ref[...], wg_ref[...], preferred_element_type=F32)
    v = jnp.dot(a_ref[...], wv_ref[...], preferred_element_type=F32)
    cw = cw_ref[...]
    conv = (g[HALO:, :] * cw[2:3, :]
            + g[HALO - 1:HALO - 1 + tm, :] * cw[1:2, :]
            + g[HALO - 2:HALO - 2 + tm, :] * cw[0:1, :]) + cb_ref[...]
    gelu = 0.5 * conv * (1.0 + lax.erf(conv * (2.0 ** -0.5)))
    o_ref[...] = (gelu * v).astype(o_ref.dtype)


def ffn_up(hn, w_up, conv_w, conv_b, seq, *, tm=1024, tn=512):
    t, k = hn.shape
    d_ff = w_up.shape[1] // 2
    n_tiles = d_ff // tn
    tm = min(tm, seq)
    kernel = functools.partial(_ffn_up_kernel, tiles_per_seq=seq // tm)
    halo_blocks = tm // HALO
    return pl.pallas_call(
        kernel,
        grid=(t // tm, n_tiles),
        in_specs=[
            pl.BlockSpec((tm, k), lambda i, j: (i, 0)),
            pl.BlockSpec((HALO, k),
                         lambda i, j: (jnp.maximum(i * halo_blocks - 1, 0), 0)),
            pl.BlockSpec((k, tn), lambda i, j: (0, j)),
            pl.BlockSpec((k, tn), lambda i, j: (0, j + n_tiles)),
            pl.BlockSpec((CONV_WIDTH, tn), lambda i, j: (0, j)),
            pl.BlockSpec((1, tn), lambda i, j: (0, j)),
        ],
        out_specs=pl.BlockSpec((tm, tn), lambda i, j: (i, j)),
        out_shape=jax.ShapeDtypeStruct((t, d_ff), ACT_DTYPE),
        scratch_shapes=[pltpu.VMEM((tm + HALO, k), hn.dtype)],
        compiler_params=_params(("parallel", "arbitrary")),
        name="ffn_up",
    )(hn, hn, w_up, w_up, conv_w, conv_b.reshape(1, d_ff))


Q_BLOCK = 128
KEY_CHUNK = 512
HEAD_GROUP = 4
INT_MIN = -2 ** 31
KEY_OF_NEG_INF = -2139095041


def _sortable(x):
    b = lax.bitcast_convert_type(x, I32)
    return b ^ ((b >> 31) & 0x7FFFFFFF)


def _dsa_layout(d_model):
    q = 0
    k = q + d_model
    v = k + LANES
    qi = v + LANES
    ki = qi + IDX_HEADS * IDX_DIM
    wi = ki + LANES
    return dict(q=q, k=k, v=v, qi=qi, ki=ki, wi=wi, total=wi + LANES)


def _dsa_kernel(proj_ref, cosa_ref, sina_ref, cosi_ref, sini_ref, qg_ref, kg_ref, kig_ref,
                out_ref,
                k_s, v_s, ki_s, q_s, qi_s, wi_s, sc_s, bias_s, o_s, m_s, l_s, cut_s,
                *, n_keep, kc, off):
    j = pl.program_id(1)
    qb = Q_BLOCK
    dh = LANES
    n_heads = q_s.shape[0] // qb
    row0 = pl.multiple_of(j * qb, qb)
    n_chunks = (row0 + qb + kc - 1) // kc
    cosa, sina, cosi, sini = cosa_ref[...], sina_ref[...], cosi_ref[...], sini_ref[...]

    def rope(x, c, s):
        return x * c + pltpu.roll(x, dh // 2, 1) * s

    def cols(start):
        return proj_ref[:, start:start + dh].astype(F32)

    @pl.when(j == 0)
    def _():
        k_s[...] = jnp.zeros_like(k_s)
        v_s[...] = jnp.zeros_like(v_s)
        ki_s[...] = jnp.zeros_like(ki_s)

    k_s[pl.ds(row0, qb), :] = rope(_rms(cols(off["k"]), kg_ref[...]), cosa, sina).astype(k_s.dtype)
    v_s[pl.ds(row0, qb), :] = cols(off["v"]).astype(v_s.dtype)
    ki_s[pl.ds(row0, qb), :] = rope(_rms(cols(off["ki"]), kig_ref[...]), cosi, sini).astype(ki_s.dtype)

    scale = dh ** -0.5
    for h in range(n_heads):
        qh = rope(_rms(cols(off["q"] + h * dh), qg_ref[...]), cosa, sina) * scale
        q_s[h * qb:(h + 1) * qb, :] = qh.astype(q_s.dtype)

    lane = lax.broadcasted_iota(I32, (qb, dh), 1)
    first_head = ((lane // (IDX_DIM // 2)) % 2) == 0
    for p in range(IDX_HEADS // 2):
        x = rope(cols(off["qi"] + p * dh), cosi, sini)
        qi_s[(2 * p) * qb:(2 * p + 1) * qb, :] = jnp.where(first_head, x, 0.0).astype(qi_s.dtype)
        qi_s[(2 * p + 1) * qb:(2 * p + 2) * qb, :] = jnp.where(first_head, 0.0, x).astype(qi_s.dtype)
    wi_s[...] = (cols(off["wi"]) * (IDX_HEADS ** -0.5 * IDX_DIM ** -0.5)).T

    q_pos = lax.broadcasted_iota(I32, (kc, qb), 1) + row0
    k_iota = lax.broadcasted_iota(I32, (kc, qb), 0)

    def idx_body(c, carry):
        k0 = pl.multiple_of(c * kc, kc)
        kic = ki_s[pl.ds(k0, kc), :]
        acc = jnp.zeros((kc, 2 * qb), F32)
        for p in range(IDX_HEADS // 2):
            r = lax.dot_general(kic, qi_s[2 * p * qb:(2 * p + 2) * qb, :], _NT,
                                preferred_element_type=F32)
            w = jnp.concatenate([wi_s[2 * p:2 * p + 1, :], wi_s[2 * p + 1:2 * p + 2, :]], axis=1)
            acc = acc + jnp.maximum(r, 0.0) * w
        s = acc[:, :qb] + acc[:, qb:]
        s = jnp.where(k_iota + k0 <= q_pos, s + 0.0, -jnp.inf)
        sc_s[pl.ds(k0, kc), :] = _sortable(s)
        return carry

    lax.fori_loop(0, n_chunks, idx_body, 0)

    def count(hits):
        def body(c, acc):
            k0 = pl.multiple_of(c * kc, kc)
            hit = hits(sc_s[pl.ds(k0, kc), :], k_iota + k0)
            return acc + jnp.sum(hit.reshape(kc // 8, 8, qb), axis=0)
        acc = lax.fori_loop(0, n_chunks, body, jnp.zeros((8, qb), I32))
        return jnp.sum(acc, axis=0, keepdims=True)

    cnt = count(lambda key, pos: jnp.where(key >= 0, 1, 0))
    thr = jnp.where(cnt >= n_keep, 0, INT_MIN).astype(I32)

    def bit_body(i, thr):
        cand = thr | jnp.left_shift(jnp.int32(1), 30 - i)
        cnt = count(lambda key, pos: jnp.where(key >= cand, 1, 0))
        return jnp.where(cnt >= n_keep, cand, thr)

    thr = lax.fori_loop(0, 31, bit_body, thr)

    n_above = count(lambda key, pos: jnp.where(key > thr, 1, 0))
    n_ties = count(lambda key, pos: jnp.where(key == thr, 1, 0))
    tie_quota = n_keep - n_above
    tied = jnp.where(thr > KEY_OF_NEG_INF, jnp.where(n_ties > tie_quota, 1, 0), 0)
    n_pos = k_s.shape[0]
    cut_s[...] = jnp.full(cut_s.shape, n_pos, I32)

    @pl.when(jnp.max(tied.astype(F32)) > 0.5)
    def _():
        pos_bits = int(n_pos - 1).bit_length()

        def pos_body(i, cpos):
            cand = cpos | jnp.left_shift(jnp.int32(1), pos_bits - 1 - i)
            before = count(lambda key, pos: jnp.where(key == thr, jnp.where(pos < cand, 1, 0), 0))
            return jnp.where(before < tie_quota, cand, cpos)

        cpos = lax.fori_loop(0, pos_bits, pos_body, jnp.zeros((1, qb), I32))
        cut_s[0:1, :] = jnp.where(tied > 0, cpos, n_pos)

    cutoff = cut_s[0:1, :]

    def bias_body(c, carry):
        k0 = pl.multiple_of(c * kc, kc)
        key = sc_s[pl.ds(k0, kc), :]
        pos = k_iota + k0
        keep_tie = jnp.where(pos <= cutoff, 0.0, NEG_BIG)
        b = jnp.where(key > thr, 0.0, jnp.where(key == thr, keep_tie, NEG_BIG))
        b = jnp.where(pos <= q_pos, b, NEG_BIG)
        bias_s[:, pl.ds(k0, kc)] = b.T
        return carry

    lax.fori_loop(0, n_chunks, bias_body, 0)

    m_s[...] = jnp.full(m_s.shape, NEG_BIG, F32)
    l_s[...] = jnp.zeros_like(l_s)
    o_s[...] = jnp.zeros_like(o_s)
    rows = HEAD_GROUP * qb

    def att_body(c, carry):
        k0 = pl.multiple_of(c * kc, kc)
        kc_blk = k_s[pl.ds(k0, kc), :]
        vc_blk = v_s[pl.ds(k0, kc), :]
        bias = bias_s[:, pl.ds(k0, kc)]
        for g in range(n_heads // HEAD_GROUP):
            sl = slice(g * rows, (g + 1) * rows)
            s = lax.dot_general(q_s[sl, :], kc_blk, _NT, preferred_element_type=F32)
            s = (s.reshape(HEAD_GROUP, qb, kc) + bias[None]).reshape(rows, kc)
            m_prev = m_s[sl, :]
            m_new = jnp.maximum(m_prev, jnp.max(s, axis=1, keepdims=True))
            p = jnp.exp(s - jnp.concatenate([m_new] * (kc // LANES), axis=1))
            alpha = jnp.exp(m_prev - m_new)
            l_s[sl, :] = alpha * l_s[sl, :] + jnp.sum(p, axis=1, keepdims=True)
            o_s[sl, :] = alpha * o_s[sl, :] + jnp.dot(p.astype(vc_blk.dtype), vc_blk,
                                                       preferred_element_type=F32)
            m_s[sl, :] = m_new
        return carry

    lax.fori_loop(0, n_chunks, att_body, 0)

    for h in range(n_heads):
        sl = slice(h * qb, (h + 1) * qb)
        out_ref[:, h * dh:(h + 1) * dh] = (o_s[sl, :] / l_s[sl, :]).astype(out_ref.dtype)


def _rope_tables(seq, dim):
    inv = 1.0 / (ROPE_THETA ** (jnp.arange(0, dim, 2, dtype=F32) / dim))
    ang = jnp.arange(seq, dtype=F32)[:, None] * inv[None, :]
    return jnp.cos(ang), jnp.sin(ang)


def _dsa_tables(seq):
    ca, sa = _rope_tables(seq, LANES)
    ci, si = _rope_tables(seq, IDX_DIM)
    cos_a = jnp.concatenate([ca, ca], axis=1)
    sin_a = jnp.concatenate([-sa, sa], axis=1)
    cos_i = jnp.concatenate([ci, ci, ci, ci], axis=1)
    sin_i = jnp.concatenate([-si, -si, si, si], axis=1)
    return cos_a, sin_a, cos_i, sin_i


def _dsa_weight_columns(d_model):
    half = IDX_DIM // 2
    a_off_k = d_model
    a_off_v = a_off_k + LANES
    a_off_qi = a_off_v + LANES
    a_off_ki = a_off_qi + IDX_HEADS * IDX_DIM
    a_off_wi = a_off_ki + IDX_DIM
    lane = np.arange(LANES)
    dim = lane % half + half * (lane // (2 * half))
    head_in_pair = (lane // half) % 2
    cols = [np.arange(a_off_qi)]
    for p in range(IDX_HEADS // 2):
        cols.append(a_off_qi + (2 * p + head_in_pair) * IDX_DIM + dim)
    cols.append(a_off_ki + dim)
    cols.append(np.where(lane < IDX_HEADS, a_off_wi + lane, -1))
    return np.concatenate(cols), dim


def dsa_attention(proj, q_gain, k_gain, idx_k_gain_dup, bsz, seq, d_model, *, kc=KEY_CHUNK):
    off = _dsa_layout(d_model)
    n_q = seq // Q_BLOCK
    kc = min(kc, seq)
    n_keep = min(TOPK_MAX, seq // 4)
    n_heads = d_model // LANES
    cos_a, sin_a, cos_i, sin_i = _dsa_tables(seq)
    kernel = functools.partial(_dsa_kernel, n_keep=n_keep, kc=kc, off=off)
    table = pl.BlockSpec((Q_BLOCK, LANES), lambda b, j: (j, 0))
    gain = pl.BlockSpec((1, LANES), lambda b, j: (0, 0))
    mxu = MXU_DTYPE
    return pl.pallas_call(
        kernel,
        grid=(bsz, n_q),
        in_specs=[pl.BlockSpec((Q_BLOCK, off["total"]), lambda b, j: (b * n_q + j, 0)),
                  table, table, table, table, gain, gain, gain],
        out_specs=pl.BlockSpec((Q_BLOCK, d_model), lambda b, j: (b * n_q + j, 0)),
        out_shape=jax.ShapeDtypeStruct((bsz * seq, d_model), ACT_DTYPE),
        scratch_shapes=[
            pltpu.VMEM((seq, LANES), mxu),
            pltpu.VMEM((seq, LANES), mxu),
            pltpu.VMEM((seq, LANES), mxu),
            pltpu.VMEM((n_heads * Q_BLOCK, LANES), mxu),
            pltpu.VMEM((IDX_HEADS * Q_BLOCK, LANES), mxu),
            pltpu.VMEM((LANES, Q_BLOCK), F32),
            pltpu.VMEM((seq, Q_BLOCK), I32),
            pltpu.VMEM((Q_BLOCK, seq), F32),
            pltpu.VMEM((n_heads * Q_BLOCK, LANES), F32),
            pltpu.VMEM((n_heads * Q_BLOCK, LANES), F32),
            pltpu.VMEM((n_heads * Q_BLOCK, LANES), F32),
            pltpu.VMEM((8, Q_BLOCK), I32),
        ],
        compiler_params=_params(("parallel", "arbitrary")),
        name="dsa_attention",
    )(proj, cos_a, sin_a, cos_i, sin_i, q_gain.reshape(1, LANES), k_gain.reshape(1, LANES),
      idx_k_gain_dup.reshape(1, LANES))


def dsa_layer(h, hn, w_in, q_gain, k_gain, idx_k_gain, w_out, next_gain, bsz, seq, *, kc=KEY_CHUNK):
    d_model = h.shape[1]
    src, dim = _dsa_weight_columns(d_model)
    w = jnp.where(src[None, :] >= 0, jnp.take(w_in, np.maximum(src, 0), axis=1), 0.0).astype(MXU_DTYPE)
    proj = project(hn, w, ACT_DTYPE, tn=512)
    att = dsa_attention(proj, q_gain, k_gain, idx_k_gain[dim], bsz, seq, d_model, kc=kc)
    return out_project(att, w_out.astype(MXU_DTYPE), h, next_gain)


MLSTM_CHUNK = 256
_HI = lax.Precision.HIGHEST


def _mlstm_kernel(proj_ref, gc_ref, gr_ref, bc_ref, br_ref, hg_ref, out_ref,
                  c_s, n_s, m_s, *, dk, dv):
    cl = proj_ref.shape[0]
    nh = B_HEADS
    off_k = nh * dk
    off_v = 2 * nh * dk
    off_o = off_v + nh * dv

    @pl.when(pl.program_id(1) == 0)
    def _():
        c_s[...] = jnp.zeros_like(c_s)
        n_s[...] = jnp.zeros_like(n_s)
        m_s[...] = jnp.full(m_s.shape, M_INIT, F32)

    g_col = gc_ref[...] + bc_ref[...]
    g_row = gr_ref[...] + br_ref[...]
    row_i = lax.broadcasted_iota(I32, (cl, cl), 0)
    col_i = lax.broadcasted_iota(I32, (cl, cl), 1)
    causal = col_i <= row_i
    lower = jnp.where(causal, 1.0, 0.0).astype(F32)
    upper = jnp.where(row_i <= col_i, 1.0, 0.0).astype(F32)
    b_col = jnp.dot(lower, jax.nn.log_sigmoid(g_col), precision=_HI,
                    preferred_element_type=F32)
    b_row = jnp.dot(jax.nn.log_sigmoid(g_row[0:2 * nh, :]), upper, precision=_HI,
                    preferred_element_type=F32)

    for h in range(nh):
        q = proj_ref[:, h * dk:(h + 1) * dk]
        k = (proj_ref[:, off_k + h * dk:off_k + (h + 1) * dk].astype(F32) * (dk ** -0.5))
        v = proj_ref[:, off_v + h * dv:off_v + (h + 1) * dv]
        o = proj_ref[:, off_o + h * dv:off_o + (h + 1) * dv].astype(F32)
        bc = b_col[:, nh + h:nh + h + 1]
        br = b_row[nh + h:nh + h + 1, :]
        li_c = g_col[:, h:h + 1]
        li_r = g_row[h:h + 1, :]
        m_prev = m_s[h:h + 1, 0:1]
        c_mat = c_s[h]
        n_vec = n_s[h:h + 1, :]

        a = bc + m_prev
        dmat = jnp.where(causal, bc - br + li_r, -jnp.inf)
        m_t = jnp.maximum(a, jnp.max(dmat, axis=1, keepdims=True))
        inter = jnp.exp(a - m_t)
        s = lax.dot_general(q.astype(MXU_DTYPE), k.astype(MXU_DTYPE), _NT,
                            preferred_element_type=F32) * jnp.exp(dmat - m_t)
        num = (inter * jnp.dot(q.astype(MXU_DTYPE), c_mat.astype(MXU_DTYPE),
                               preferred_element_type=F32)
               + jnp.dot(s.astype(MXU_DTYPE), v.astype(MXU_DTYPE), preferred_element_type=F32))
        den = (inter * jnp.sum(q.astype(F32) * n_vec, axis=1, keepdims=True)
               + jnp.sum(s, axis=1, keepdims=True))
        h_t = num / jnp.maximum(jnp.abs(den), jnp.exp(-m_t))

        b_end = bc[cl - 1:cl, :]
        g_r = b_end - br + li_r
        g_c = b_end - bc + li_c
        m_new = jnp.maximum(b_end + m_prev, jnp.max(g_r, axis=1, keepdims=True))
        decay = jnp.exp(b_end + m_prev - m_new)
        kw = k * jnp.exp(g_c - m_new)
        c_s[h] = decay * c_mat + lax.dot_general(kw.astype(MXU_DTYPE), v.astype(MXU_DTYPE), _TN,
                                                 preferred_element_type=F32)
        n_s[h:h + 1, :] = decay * n_vec + jnp.sum(kw, axis=0, keepdims=True)
        m_s[h:h + 1, :] = jnp.broadcast_to(m_new, (1, m_s.shape[1]))

        hn = _rms(h_t, hg_ref[:, h * dv:(h + 1) * dv])
        out_ref[:, h * dv:(h + 1) * dv] = (hn * jax.nn.sigmoid(o)).astype(out_ref.dtype)


def mlstm_cell(proj, gates, gate_bias, head_gain, bsz, seq, d_model):
    nh = B_HEADS
    dv = d_model // nh
    dk = dv // 2
    cl = min(MLSTM_CHUNK, seq)
    nc = seq // cl
    width = proj.shape[1]
    bias = jnp.zeros((LANES,), F32).at[:2 * nh].set(gate_bias.astype(F32))
    kernel = functools.partial(_mlstm_kernel, dk=dk, dv=dv)
    return pl.pallas_call(
        kernel,
        grid=(bsz, nc),
        in_specs=[pl.BlockSpec((cl, width), lambda b, c: (b * nc + c, 0)),
                  pl.BlockSpec((cl, LANES), lambda b, c: (b * nc + c, 0)),
                  pl.BlockSpec((LANES, cl), lambda b, c: (0, b * nc + c)),
                  pl.BlockSpec((1, LANES), lambda b, c: (0, 0)),
                  pl.BlockSpec((LANES, 1), lambda b, c: (0, 0)),
                  pl.BlockSpec((1, d_model), lambda b, c: (0, 0))],
        out_specs=pl.BlockSpec((cl, d_model), lambda b, c: (b * nc + c, 0)),
        out_shape=jax.ShapeDtypeStruct((bsz * seq, d_model), ACT_DTYPE),
        scratch_shapes=[pltpu.VMEM((nh, dk, dv), F32),
                        pltpu.VMEM((nh, dk), F32),
                        pltpu.VMEM((nh, LANES), F32)],
        compiler_params=_params(("parallel", "arbitrary")),
        name="mlstm_cell",
    )(proj, gates, gates.T, bias.reshape(1, LANES), bias.reshape(LANES, 1),
      head_gain.reshape(1, d_model))


def mlstm_layer(h, hn, w_in, gate_bias, head_gain, w_out, next_gain, bsz, seq):
    d_model = h.shape[1]
    nh = B_HEADS
    off_g = 3 * d_model
    w_main = w_in[:, :off_g].astype(MXU_DTYPE)
    w_gate = jnp.pad(w_in[:, off_g:], ((0, 0), (0, LANES - 2 * nh))).astype(MXU_DTYPE)
    proj = project(hn, w_main, ACT_DTYPE, tn=512)
    gates = project(hn, w_gate, F32, tn=LANES)
    cell = mlstm_cell(proj, gates, gate_bias, head_gain, bsz, seq, d_model)
    return out_project(cell, w_out.astype(MXU_DTYPE), h, next_gain)


def kernel(x, a_norm, a_w_in, a_q_gain, a_k_gain, a_idx_k_gain, a_w_out, b_norm, b_w_in, b_gate_bias,
           b_head_norm, b_w_out, f_norm, f_w_up, f_conv_w, f_conv_b, f_w_down):
    bsz, seq, d_model = x.shape
    depth = f_norm.shape[0]
    h = x.reshape(bsz * seq, d_model)
    hn = rmsnorm(h, a_norm[0])
    for i in range(depth):
        j = i // 2
        if i % 2 == 0:
            h, hn = dsa_layer(h, hn, a_w_in[j], a_q_gain[j], a_k_gain[j], a_idx_k_gain[j], a_w_out[j],
                              f_norm[i], bsz, seq)
        else:
            h, hn = mlstm_layer(h, hn, b_w_in[j], b_gate_bias[j], b_head_norm[j], b_w_out[j],
                                f_norm[i], bsz, seq)
        act = ffn_up(hn, f_w_up[i].astype(MXU_DTYPE), f_conv_w[i], f_conv_b[i], seq)
        if i + 1 < depth:
            mixer_gain = a_norm[(i + 1) // 2] if (i + 1) % 2 == 0 else b_norm[(i + 1) // 2]
            h, hn = out_project(act, f_w_down[i].astype(MXU_DTYPE), h, mixer_gain)
        else:
            h = out_project(act, f_w_down[i].astype(MXU_DTYPE), h, None)
    return h.reshape(bsz, seq, d_model)
```

```python
import functools

import jax
import jax.numpy as jnp
import numpy as np
from jax import lax
from jax.experimental import pallas as pl
from jax.experimental.pallas import tpu as pltpu

EPS = 1e-6
ROPE_THETA = 10000.0
A_HEADS = 16
IDX_HEADS = 16
IDX_DIM = 64
TOPK_MAX = 256
B_HEADS = 8
CONV_WIDTH = 3
M_INIT = -1e30

LANES = 128
BF16_SUBLANES = 16
VMEM_LIMIT_BYTES = 56 * 1024 * 1024

MXU_DTYPE = jnp.bfloat16
ACT_DTYPE = jnp.bfloat16
NEG_BIG = -1e30

F32 = jnp.float32
I32 = jnp.int32
_NT = (((1,), (1,)), ((), ()))
_TN = (((0,), (0,)), ((), ()))


def _params(semantics):
    return pltpu.CompilerParams(dimension_semantics=semantics,
                                vmem_limit_bytes=VMEM_LIMIT_BYTES)


def _rms(x, gain):
    ms = jnp.mean(x * x, axis=-1, keepdims=True)
    return x * lax.rsqrt(ms + EPS) * gain


def _rmsnorm_kernel(x_ref, g_ref, o_ref):
    o_ref[...] = _rms(x_ref[...], g_ref[...]).astype(o_ref.dtype)


def rmsnorm(x, gain, *, tm=512):
    t, d = x.shape
    return pl.pallas_call(
        _rmsnorm_kernel,
        grid=(t // tm,),
        in_specs=[pl.BlockSpec((tm, d), lambda i: (i, 0)),
                  pl.BlockSpec((1, d), lambda i: (0, 0))],
        out_specs=pl.BlockSpec((tm, d), lambda i: (i, 0)),
        out_shape=jax.ShapeDtypeStruct((t, d), ACT_DTYPE),
        compiler_params=_params(("parallel",)),
        name="rmsnorm",
    )(x, gain.reshape(1, d))


def _proj_kernel(a_ref, w_ref, o_ref):
    o_ref[...] = jnp.dot(a_ref[...], w_ref[...],
                         preferred_element_type=F32).astype(o_ref.dtype)


def project(a, w, out_dtype, *, tm=1024, tn=512):
    t, k = a.shape
    n = w.shape[1]
    tm = min(tm, t)
    tn = min(tn, n)
    return pl.pallas_call(
        _proj_kernel,
        grid=(t // tm, n // tn),
        in_specs=[pl.BlockSpec((tm, k), lambda i, j: (i, 0)),
                  pl.BlockSpec((k, tn), lambda i, j: (0, j))],
        out_specs=pl.BlockSpec((tm, tn), lambda i, j: (i, j)),
        out_shape=jax.ShapeDtypeStruct((t, n), out_dtype),
        compiler_params=_params(("parallel", "arbitrary")),
        name="project",
    )(a, w)


OUT_COLS = 512
NORM_ROWS = 256


def _out_proj_kernel(a_ref, w_ref, r_ref, g_ref, h_ref, hn_ref):
    kk = pl.program_id(1)

    @pl.when(kk == 0)
    def _():
        h_ref[...] = r_ref[...]

    a = a_ref[...]
    for n0 in range(0, h_ref.shape[1], OUT_COLS):
        h_ref[:, n0:n0 + OUT_COLS] += jnp.dot(a, w_ref[:, n0:n0 + OUT_COLS],
                                              preferred_element_type=F32)

    if hn_ref is not None:
        @pl.when(kk == pl.num_programs(1) - 1)
        def _():
            for r0 in range(0, h_ref.shape[0], NORM_ROWS):
                rows = slice(r0, r0 + NORM_ROWS)
                hn_ref[rows, :] = _rms(h_ref[rows, :], g_ref[...]).astype(hn_ref.dtype)


def _out_proj_last_kernel(a_ref, w_ref, r_ref, h_ref):
    _out_proj_kernel(a_ref, w_ref, r_ref, None, h_ref, None)


def out_project(a, w, resid, next_gain):
    t, k = a.shape
    d = w.shape[1]
    tm, tk = (512, k) if k <= 2048 else (1024, 512)
    tm = min(tm, t)
    grid = (t // tm, k // tk)
    a_spec = pl.BlockSpec((tm, tk), lambda i, kk: (i, kk))
    w_spec = pl.BlockSpec((tk, d), lambda i, kk: (kk, 0))
    row_spec = pl.BlockSpec((tm, d), lambda i, kk: (i, 0))
    if next_gain is None:
        return pl.pallas_call(
            _out_proj_last_kernel, grid=grid,
            in_specs=[a_spec, w_spec, row_spec],
            out_specs=row_spec,
            out_shape=jax.ShapeDtypeStruct((t, d), F32),
            compiler_params=_params(("parallel", "arbitrary")),
            name="out_project_last",
        )(a, w, resid)
    return pl.pallas_call(
        _out_proj_kernel, grid=grid,
        in_specs=[a_spec, w_spec, row_spec,
                  pl.BlockSpec((1, d), lambda i, kk: (0, 0))],
        out_specs=[row_spec, row_spec],
        out_shape=[jax.ShapeDtypeStruct((t, d), F32),
                   jax.ShapeDtypeStruct((t, d), ACT_DTYPE)],
        compiler_params=_params(("parallel", "arbitrary")),
        name="out_project",
    )(a, w, resid, next_gain.reshape(1, d))


HALO = BF16_SUBLANES


def _ffn_up_kernel(a_ref, halo_ref, wg_ref, wv_ref, cw_ref, cb_ref, o_ref,
                   lhs_ref, *, tiles_per_seq):
    i = pl.program_id(0)
    tm = a_ref.shape[0]

    @pl.when(pl.program_id(1) == 0)
    def _():
        seq_start = (i % tiles_per_seq) == 0
        halo = halo_ref[...]
        lhs_ref[0:HALO, :] = jnp.where(seq_start, jnp.zeros_like(halo), halo)
        lhs_ref[HALO:, :] = a_ref[...]

    g = jnp.dot(lhs_ref[...], wg_ref[...], preferred_element_type=F32)
    v = jnp.dot(a_ref[...], wv_ref[...], preferred_element_type=F32)
    cw = cw_ref[...]
    conv = (g[HALO:, :] * cw[2:3, :]
            + g[HALO - 1:HALO - 1 + tm, :] * cw[1:2, :]
            + g[HALO - 2:HALO - 2 + tm, :] * cw[0:1, :]) + cb_ref[...]
    gelu = 0.5 * conv * (1.0 + lax.erf(conv * (2.0 ** -0.5)))
    o_ref[...] = (gelu * v).astype(o_ref.dtype)


def ffn_up(hn, w_up, conv_w, conv_b, seq, *, tm=1024, tn=512):
    t, k = hn.shape
    d_ff = w_up.shape[1] // 2
    n_tiles = d_ff // tn
    tm = min(tm, seq)
    kernel = functools.partial(_ffn_up_kernel, tiles_per_seq=seq // tm)
    halo_blocks = tm // HALO
    return pl.pallas_call(
        kernel,
        grid=(t // tm, n_tiles),
        in_specs=[
            pl.BlockSpec((tm, k), lambda i, j: (i, 0)),
            pl.BlockSpec((HALO, k),
                         lambda i, j: (jnp.maximum(i * halo_blocks - 1, 0), 0)),
            pl.BlockSpec((k, tn), lambda i, j: (0, j)),
            pl.BlockSpec((k, tn), lambda i, j: (0, j + n_tiles)),
            pl.BlockSpec((CONV_WIDTH, tn), lambda i, j: (0, j)),
            pl.BlockSpec((1, tn), lambda i, j: (0, j)),
        ],
        out_specs=pl.BlockSpec((tm, tn), lambda i, j: (i, j)),
        out_shape=jax.ShapeDtypeStruct((t, d_ff), ACT_DTYPE),
        scratch_shapes=[pltpu.VMEM((tm + HALO, k), hn.dtype)],
        compiler_params=_params(("parallel", "arbitrary")),
        name="ffn_up",
    )(hn, hn, w_up, w_up, conv_w, conv_b.reshape(1, d_ff))


Q_BLOCK = 128
KEY_CHUNK = 512
HEAD_GROUP = 4
INT_MIN = -2 ** 31
LOG2_E = 1.4426950408889634
KEY_OF_NEG_INF = -2139095041


def _sortable(x):
    b = lax.bitcast_convert_type(x, I32)
    return b ^ ((b >> 31) & 0x7FFFFFFF)


def _dsa_layout(d_model):
    q = 0
    k = q + d_model
    v = k + LANES
    qi = v + LANES
    ki = qi + IDX_HEADS * IDX_DIM
    wi = ki + LANES
    return dict(q=q, k=k, v=v, qi=qi, ki=ki, wi=wi, total=wi + LANES)


def _dsa_kernel(proj_ref, cosa_ref, sina_ref, cosi_ref, sini_ref, qg_ref, kg_ref, kig_ref,
                out_ref,
                k_s, v_s, ki_s, q_s, qi_s, wi_s, sc_s, bias_s, o_s, m_s, l_s, cut_s,
                *, n_keep, kc, off):
    j = pl.program_id(1)
    qb = Q_BLOCK
    dh = LANES
    n_heads = q_s.shape[0] // qb
    row0 = pl.multiple_of(j * qb, qb)
    n_chunks = (row0 + qb + kc - 1) // kc
    cosa, sina, cosi, sini = cosa_ref[...], sina_ref[...], cosi_ref[...], sini_ref[...]

    def rope(x, c, s):
        return x * c + pltpu.roll(x, dh // 2, 1) * s

    def cols(start):
        return proj_ref[:, start:start + dh].astype(F32)

    @pl.when(j == 0)
    def _():
        k_s[...] = jnp.zeros_like(k_s)
        v_s[...] = jnp.zeros_like(v_s)
        ki_s[...] = jnp.zeros_like(ki_s)

    k_s[pl.ds(row0, qb), :] = rope(_rms(cols(off["k"]), kg_ref[...]), cosa, sina).astype(k_s.dtype)
    v_s[pl.ds(row0, qb), 0:dh] = cols(off["v"]).astype(v_s.dtype)
    v_s[pl.ds(row0, qb), dh:2 * dh] = jnp.ones((qb, dh), v_s.dtype)
    ki_s[pl.ds(row0, qb), :] = rope(_rms(cols(off["ki"]), kig_ref[...]), cosi, sini).astype(ki_s.dtype)

    scale = dh ** -0.5 * LOG2_E
    for h in range(n_heads):
        qh = rope(_rms(cols(off["q"] + h * dh), qg_ref[...]), cosa, sina) * scale
        q_s[h * qb:(h + 1) * qb, :] = qh.astype(q_s.dtype)

    lane = lax.broadcasted_iota(I32, (qb, dh), 1)
    first_head = ((lane // (IDX_DIM // 2)) % 2) == 0
    for p in range(IDX_HEADS // 2):
        x = rope(cols(off["qi"] + p * dh), cosi, sini)
        qi_s[(2 * p) * qb:(2 * p + 1) * qb, :] = jnp.where(first_head, x, 0.0).astype(qi_s.dtype)
        qi_s[(2 * p + 1) * qb:(2 * p + 2) * qb, :] = jnp.where(first_head, 0.0, x).astype(qi_s.dtype)
    wi_s[...] = (cols(off["wi"]) * (IDX_HEADS ** -0.5 * IDX_DIM ** -0.5)).T

    q_pos = lax.broadcasted_iota(I32, (kc, qb), 1) + row0
    k_iota = lax.broadcasted_iota(I32, (kc, qb), 0)

    def idx_body(c, carry):
        k0 = pl.multiple_of(c * kc, kc)
        kic = ki_s[pl.ds(k0, kc), :]
        acc = jnp.zeros((kc, 2 * qb), F32)
        for p in range(IDX_HEADS // 2):
            r = lax.dot_general(kic, qi_s[2 * p * qb:(2 * p + 2) * qb, :], _NT,
                                preferred_element_type=F32)
            w = jnp.concatenate([wi_s[2 * p:2 * p + 1, :], wi_s[2 * p + 1:2 * p + 2, :]], axis=1)
            acc = acc + jnp.maximum(r, 0.0) * w
        s = acc[:, :qb] + acc[:, qb:]
        s = jnp.where(k_iota + k0 <= q_pos, s + 0.0, -jnp.inf)
        sc_s[pl.ds(k0, kc), :] = _sortable(s)
        return carry

    lax.fori_loop(0, n_chunks, idx_body, 0)

    def count(hits):
        def body(c, acc):
            k0 = pl.multiple_of(c * kc, kc)
            hit = hits(sc_s[pl.ds(k0, kc), :], k_iota + k0)
            return acc + jnp.sum(hit.reshape(kc // 8, 8, qb), axis=0)
        acc = lax.fori_loop(0, n_chunks, body, jnp.zeros((8, qb), I32))
        return jnp.sum(acc, axis=0, keepdims=True)

    cnt = count(lambda key, pos: jnp.where(key >= 0, 1, 0))
    thr = jnp.where(cnt >= n_keep, 0, INT_MIN).astype(I32)

    def bit_body(i, thr):
        cand = thr | jnp.left_shift(jnp.int32(1), 30 - i)
        cnt = count(lambda key, pos: jnp.where(key >= cand, 1, 0))
        return jnp.where(cnt >= n_keep, cand, thr)

    thr = lax.fori_loop(0, 31, bit_body, thr)

    n_above = count(lambda key, pos: jnp.where(key > thr, 1, 0))
    n_ties = count(lambda key, pos: jnp.where(key == thr, 1, 0))
    tie_quota = n_keep - n_above
    tied = jnp.where(thr > KEY_OF_NEG_INF, jnp.where(n_ties > tie_quota, 1, 0), 0)
    n_pos = k_s.shape[0]
    cut_s[...] = jnp.full(cut_s.shape, n_pos, I32)

    @pl.when(jnp.max(tied.astype(F32)) > 0.5)
    def _():
        pos_bits = int(n_pos - 1).bit_length()

        def pos_body(i, cpos):
            cand = cpos | jnp.left_shift(jnp.int32(1), pos_bits - 1 - i)
            before = count(lambda key, pos: jnp.where(key == thr, jnp.where(pos < cand, 1, 0), 0))
            return jnp.where(before < tie_quota, cand, cpos)

        cpos = lax.fori_loop(0, pos_bits, pos_body, jnp.zeros((1, qb), I32))
        cut_s[0:1, :] = jnp.where(tied > 0, cpos, n_pos)

    cutoff = cut_s[0:1, :]

    def bias_body(c, carry):
        k0 = pl.multiple_of(c * kc, kc)
        key = sc_s[pl.ds(k0, kc), :]
        pos = k_iota + k0
        keep_tie = jnp.where(pos <= cutoff, 0.0, NEG_BIG)
        b = jnp.where(key > thr, 0.0, jnp.where(key == thr, keep_tie, NEG_BIG))
        b = jnp.where(pos <= q_pos, b, NEG_BIG)
        bias_s[:, pl.ds(k0, kc)] = b.T
        return carry

    lax.fori_loop(0, n_chunks, bias_body, 0)

    m_s[...] = jnp.full(m_s.shape, NEG_BIG, F32)
    l_s[...] = jnp.zeros_like(l_s)
    o_s[...] = jnp.zeros_like(o_s)
    rows = HEAD_GROUP * qb

    def att_body(c, carry):
        k0 = pl.multiple_of(c * kc, kc)
        kc_blk = k_s[pl.ds(k0, kc), :]
        vc_blk = v_s[pl.ds(k0, kc), :]
        bias = bias_s[:, pl.ds(k0, kc)]
        for g in range(n_heads // HEAD_GROUP):
            sl = slice(g * rows, (g + 1) * rows)
            s = lax.dot_general(q_s[sl, :], kc_blk, _NT, preferred_element_type=F32)
            s = (s.reshape(HEAD_GROUP, qb, kc) + bias[None]).reshape(rows, kc)
            m_prev = m_s[sl, :]
            m_new = jnp.maximum(m_prev, jnp.max(s, axis=1, keepdims=True))
            p = jnp.exp2(s - jnp.concatenate([m_new] * (kc // LANES), axis=1))
            alpha = jnp.exp2(m_prev - m_new)
            pv = jnp.dot(p.astype(vc_blk.dtype), vc_blk, preferred_element_type=F32)
            o_s[sl, :] = alpha * o_s[sl, :] + pv[:, 0:dh]
            l_s[sl, :] = alpha * l_s[sl, :] + pv[:, dh:2 * dh]
            m_s[sl, :] = m_new
        return carry

    lax.fori_loop(0, n_chunks, att_body, 0)

    for h in range(n_heads):
        sl = slice(h * qb, (h + 1) * qb)
        out_ref[:, h * dh:(h + 1) * dh] = (o_s[sl, :] / l_s[sl, :]).astype(out_ref.dtype)


def _rope_tables(seq, dim):
    inv = 1.0 / (ROPE_THETA ** (jnp.arange(0, dim, 2, dtype=F32) / dim))
    ang = jnp.arange(seq, dtype=F32)[:, None] * inv[None, :]
    return jnp.cos(ang), jnp.sin(ang)


def _dsa_tables(seq):
    ca, sa = _rope_tables(seq, LANES)
    ci, si = _rope_tables(seq, IDX_DIM)
    cos_a = jnp.concatenate([ca, ca], axis=1)
    sin_a = jnp.concatenate([-sa, sa], axis=1)
    cos_i = jnp.concatenate([ci, ci, ci, ci], axis=1)
    sin_i = jnp.concatenate([-si, -si, si, si], axis=1)
    return cos_a, sin_a, cos_i, sin_i


def _dsa_weight_columns(d_model):
    half = IDX_DIM // 2
    a_off_k = d_model
    a_off_v = a_off_k + LANES
    a_off_qi = a_off_v + LANES
    a_off_ki = a_off_qi + IDX_HEADS * IDX_DIM
    a_off_wi = a_off_ki + IDX_DIM
    lane = np.arange(LANES)
    dim = lane % half + half * (lane // (2 * half))
    head_in_pair = (lane // half) % 2
    cols = [np.arange(a_off_qi)]
    for p in range(IDX_HEADS // 2):
        cols.append(a_off_qi + (2 * p + head_in_pair) * IDX_DIM + dim)
    cols.append(a_off_ki + dim)
    cols.append(np.where(lane < IDX_HEADS, a_off_wi + lane, -1))
    return np.concatenate(cols), dim


def dsa_attention(proj, q_gain, k_gain, idx_k_gain_dup, bsz, seq, d_model, *, kc=KEY_CHUNK):
    off = _dsa_layout(d_model)
    n_q = seq // Q_BLOCK
    kc = min(kc, seq)
    n_keep = min(TOPK_MAX, seq // 4)
    n_heads = d_model // LANES
    cos_a, sin_a, cos_i, sin_i = _dsa_tables(seq)
    kernel = functools.partial(_dsa_kernel, n_keep=n_keep, kc=kc, off=off)
    table = pl.BlockSpec((Q_BLOCK, LANES), lambda b, j: (j, 0))
    gain = pl.BlockSpec((1, LANES), lambda b, j: (0, 0))
    mxu = MXU_DTYPE
    return pl.pallas_call(
        kernel,
        grid=(bsz, n_q),
        in_specs=[pl.BlockSpec((Q_BLOCK, off["total"]), lambda b, j: (b * n_q + j, 0)),
                  table, table, table, table, gain, gain, gain],
        out_specs=pl.BlockSpec((Q_BLOCK, d_model), lambda b, j: (b * n_q + j, 0)),
        out_shape=jax.ShapeDtypeStruct((bsz * seq, d_model), ACT_DTYPE),
        scratch_shapes=[
            pltpu.VMEM((seq, LANES), mxu),
            pltpu.VMEM((seq, 2 * LANES), mxu),
            pltpu.VMEM((seq, LANES), mxu),
            pltpu.VMEM((n_heads * Q_BLOCK, LANES), mxu),
            pltpu.VMEM((IDX_HEADS * Q_BLOCK, LANES), mxu),
            pltpu.VMEM((LANES, Q_BLOCK), F32),
            pltpu.VMEM((seq, Q_BLOCK), I32),
            pltpu.VMEM((Q_BLOCK, seq), F32),
            pltpu.VMEM((n_heads * Q_BLOCK, LANES), F32),
            pltpu.VMEM((n_heads * Q_BLOCK, LANES), F32),
            pltpu.VMEM((n_heads * Q_BLOCK, LANES), F32),
            pltpu.VMEM((8, Q_BLOCK), I32),
        ],
        compiler_params=_params(("parallel", "arbitrary")),
        name="dsa_attention",
    )(proj, cos_a, sin_a, cos_i, sin_i, q_gain.reshape(1, LANES), k_gain.reshape(1, LANES),
      idx_k_gain_dup.reshape(1, LANES))


def dsa_layer(h, hn, w_in, q_gain, k_gain, idx_k_gain, w_out, next_gain, bsz, seq, *, kc=KEY_CHUNK):
    d_model = h.shape[1]
    src, dim = _dsa_weight_columns(d_model)
    w = jnp.where(src[None, :] >= 0, jnp.take(w_in, np.maximum(src, 0), axis=1), 0.0).astype(MXU_DTYPE)
    proj = project(hn, w, ACT_DTYPE, tn=512)
    att = dsa_attention(proj, q_gain, k_gain, idx_k_gain[dim], bsz, seq, d_model, kc=kc)
    return out_project(att, w_out.astype(MXU_DTYPE), h, next_gain)


MLSTM_CHUNK = 256
_HI = lax.Precision.HIGHEST


def _mlstm_kernel(proj_ref, gc_ref, gr_ref, bc_ref, br_ref, hg_ref, out_ref,
                  c_s, n_s, m_s, *, dk, dv):
    cl = proj_ref.shape[0]
    nh = B_HEADS
    off_k = nh * dk
    off_v = 2 * nh * dk
    off_o = off_v + nh * dv

    @pl.when(pl.program_id(1) == 0)
    def _():
        c_s[...] = jnp.zeros_like(c_s)
        n_s[...] = jnp.zeros_like(n_s)
        m_s[...] = jnp.full(m_s.shape, M_INIT, F32)

    g_col = gc_ref[...] + bc_ref[...]
    g_row = gr_ref[...] + br_ref[...]
    row_i = lax.broadcasted_iota(I32, (cl, cl), 0)
    col_i = lax.broadcasted_iota(I32, (cl, cl), 1)
    causal = col_i <= row_i
    lower = jnp.where(causal, 1.0, 0.0).astype(F32)
    upper = jnp.where(row_i <= col_i, 1.0, 0.0).astype(F32)
    b_col = jnp.dot(lower, jax.nn.log_sigmoid(g_col), precision=_HI,
                    preferred_element_type=F32)
    b_row = jnp.dot(jax.nn.log_sigmoid(g_row[0:2 * nh, :]), upper, precision=_HI,
                    preferred_element_type=F32)

    for h in range(nh):
        q = proj_ref[:, h * dk:(h + 1) * dk]
        k = (proj_ref[:, off_k + h * dk:off_k + (h + 1) * dk].astype(F32) * (dk ** -0.5))
        v = proj_ref[:, off_v + h * dv:off_v + (h + 1) * dv]
        o = proj_ref[:, off_o + h * dv:off_o + (h + 1) * dv].astype(F32)
        bc = b_col[:, nh + h:nh + h + 1]
        br = b_row[nh + h:nh + h + 1, :]
        li_c = g_col[:, h:h + 1]
        li_r = g_row[h:h + 1, :]
        m_prev = m_s[h:h + 1, 0:1]
        c_mat = c_s[h]
        n_vec = n_s[h:h + 1, :]

        a = bc + m_prev
        dmat = jnp.where(causal, bc - br + li_r, -jnp.inf)
        m_t = jnp.maximum(a, jnp.max(dmat, axis=1, keepdims=True))
        inter = jnp.exp(a - m_t)
        s = lax.dot_general(q.astype(MXU_DTYPE), k.astype(MXU_DTYPE), _NT,
                            preferred_element_type=F32) * jnp.exp(dmat - m_t)
        num = (inter * jnp.dot(q.astype(MXU_DTYPE), c_mat.astype(MXU_DTYPE),
                               preferred_element_type=F32)
               + jnp.dot(s.astype(MXU_DTYPE), v.astype(MXU_DTYPE), preferred_element_type=F32))
        den = (inter * jnp.sum(q.astype(F32) * n_vec, axis=1, keepdims=True)
               + jnp.sum(s, axis=1, keepdims=True))
        h_t = num / jnp.maximum(jnp.abs(den), jnp.exp(-m_t))

        b_end = bc[cl - 1:cl, :]
        g_r = b_end - br + li_r
        g_c = b_end - bc + li_c
        m_new = jnp.maximum(b_end + m_prev, jnp.max(g_r, axis=1, keepdims=True))
        decay = jnp.exp(b_end + m_prev - m_new)
        kw = k * jnp.exp(g_c - m_new)
        c_s[h] = decay * c_mat + lax.dot_general(kw.astype(MXU_DTYPE), v.astype(MXU_DTYPE), _TN,
                                                 preferred_element_type=F32)
        n_s[h:h + 1, :] = decay * n_vec + jnp.sum(kw, axis=0, keepdims=True)
        m_s[h:h + 1, :] = jnp.broadcast_to(m_new, (1, m_s.shape[1]))

        hn = _rms(h_t, hg_ref[:, h * dv:(h + 1) * dv])
        out_ref[:, h * dv:(h + 1) * dv] = (hn * jax.nn.sigmoid(o)).astype(out_ref.dtype)


def mlstm_cell(proj, gates, gate_bias, head_gain, bsz, seq, d_model):
    nh = B_HEADS
    dv = d_model // nh
    dk = dv // 2
    cl = min(MLSTM_CHUNK, seq)
    nc = seq // cl
    width = proj.shape[1]
    bias = jnp.zeros((LANES,), F32).at[:2 * nh].set(gate_bias.astype(F32))
    kernel = functools.partial(_mlstm_kernel, dk=dk, dv=dv)
    return pl.pallas_call(
        kernel,
        grid=(bsz, nc),
        in_specs=[pl.BlockSpec((cl, width), lambda b, c: (b * nc + c, 0)),
                  pl.BlockSpec((cl, LANES), lambda b, c: (b * nc + c, 0)),
                  pl.BlockSpec((LANES, cl), lambda b, c: (0, b * nc + c)),
                  pl.BlockSpec((1, LANES), lambda b, c: (0, 0)),
                  pl.BlockSpec((LANES, 1), lambda b, c: (0, 0)),
                  pl.BlockSpec((1, d_model), lambda b, c: (0, 0))],
        out_specs=pl.BlockSpec((cl, d_model), lambda b, c: (b * nc + c, 0)),
        out_shape=jax.ShapeDtypeStruct((bsz * seq, d_model), ACT_DTYPE),
        scratch_shapes=[pltpu.VMEM((nh, dk, dv), F32),
                        pltpu.VMEM((nh, dk), F32),
                        pltpu.VMEM((nh, LANES), F32)],
        compiler_params=_params(("parallel", "arbitrary")),
        name="mlstm_cell",
    )(proj, gates, gates.T, bias.reshape(1, LANES), bias.reshape(LANES, 1),
      head_gain.reshape(1, d_model))


def mlstm_layer(h, hn, w_in, gate_bias, head_gain, w_out, next_gain, bsz, seq):
    d_model = h.shape[1]
    nh = B_HEADS
    off_g = 3 * d_model
    w_main = w_in[:, :off_g].astype(MXU_DTYPE)
    w_gate = jnp.pad(w_in[:, off_g:], ((0, 0), (0, LANES - 2 * nh))).astype(MXU_DTYPE)
    proj = project(hn, w_main, ACT_DTYPE, tn=512)
    gates = project(hn, w_gate, F32, tn=LANES)
    cell = mlstm_cell(proj, gates, gate_bias, head_gain, bsz, seq, d_model)
    return out_project(cell, w_out.astype(MXU_DTYPE), h, next_gain)


def kernel(x, a_norm, a_w_in, a_q_gain, a_k_gain, a_idx_k_gain, a_w_out, b_norm, b_w_in, b_gate_bias,
           b_head_norm, b_w_out, f_norm, f_w_up, f_conv_w, f_conv_b, f_w_down):
    bsz, seq, d_model = x.shape
    depth = f_norm.shape[0]
    h = x.reshape(bsz * seq, d_model)
    hn = rmsnorm(h, a_norm[0])
    for i in range(depth):
        j = i // 2
        if i % 2 == 0:
            h, hn = dsa_layer(h, hn, a_w_in[j], a_q_gain[j], a_k_gain[j], a_idx_k_gain[j], a_w_out[j],
                              f_norm[i], bsz, seq)
        else:
            h, hn = mlstm_layer(h, hn, b_w_in[j], b_gate_bias[j], b_head_norm[j], b_w_out[j],
                                f_norm[i], bsz, seq)
        act = ffn_up(hn, f_w_up[i].astype(MXU_DTYPE), f_conv_w[i], f_conv_b[i], seq)
        if i + 1 < depth:
            mixer_gain = a_norm[(i + 1) // 2] if (i + 1) % 2 == 0 else b_norm[(i + 1) // 2]
            h, hn = out_project(act, f_w_down[i].astype(MXU_DTYPE), h, mixer_gain)
        else:
            h = out_project(act, f_w_down[i].astype(MXU_DTYPE), h, None)
    return h.reshape(bsz, seq, d_model)
```
